```python
import math, functools
import jax, jax.numpy as jnp
from jax import lax
import numpy as np

D_MODEL = 1024
BATCH = 2
SEQ = 8192
DEPTH = 4
DEC_BATCH = 32
DEC_SEQ = 8
PAST_LEN = 8192
PAGE_SIZE = 128

N_HEADS = D_MODEL // 128
HEAD_DIM = 64
V_DIM = 2 * HEAD_DIM
QK_WIDTH = N_HEADS * 2 * HEAD_DIM
ATTN_WIDTH = N_HEADS * V_DIM
C_CONV = D_MODEL
CONV_WIDTH = 31
N_BUCKETS = 32
MAX_DISTANCE = 128
D_FF = 2816
N_EXPERTS = 8
TOP_K = 2
D_FF_EXPERT = 3584
Q_BLOCK = 128
RMS_EPS = 1e-6
SUBLN_EPS = 1e-5
LN_EPS = 1e-5
NEG_INF = -1e30
N_DENSE = (DEPTH + 1) // 2
N_MOE = DEPTH // 2
IN_SPLITS = (2 * C_CONV,
             2 * C_CONV + QK_WIDTH,
             2 * C_CONV + 2 * QK_WIDTH,
             2 * C_CONV + 2 * QK_WIDTH + ATTN_WIDTH,
             2 * C_CONV + 2 * QK_WIDTH + ATTN_WIDTH + D_MODEL)
D_IN = IN_SPLITS[-1] + D_MODEL

kernel_name = 'gated_conformer_diffattn_moe_decode_step'


def _rmsnorm(x, g, eps=RMS_EPS):
    xf = x.astype(jnp.float32)
    y = xf * lax.rsqrt(jnp.mean(xf * xf, axis=-1, keepdims=True) + eps)
    return (y * g.astype(jnp.float32)).astype(x.dtype)


def _layernorm(x, g, b, eps=LN_EPS):
    xf = x.astype(jnp.float32)
    mu = jnp.mean(xf, axis=-1, keepdims=True)
    var = jnp.mean(jnp.square(xf - mu), axis=-1, keepdims=True)
    y = (xf - mu) * lax.rsqrt(var + eps) * g.astype(jnp.float32) + b.astype(jnp.float32)
    return y.astype(x.dtype)


def _swiglu(h, w_gate, w_up, w_down):
    return (jax.nn.silu(h @ w_gate) * (h @ w_up)) @ w_down


def _moe(h, router, w_gate, w_up, w_down):
    logits = (h @ router).astype(jnp.float32)
    top_v, top_i = lax.top_k(logits, TOP_K)
    top_w = jax.nn.softmax(top_v, axis=-1)
    gate = jnp.sum(jax.nn.one_hot(top_i, N_EXPERTS, dtype=jnp.float32) * top_w[..., None],
                   axis=-2).astype(h.dtype)
    y = jnp.zeros_like(h)
    for e in range(N_EXPERTS):
        y = y + gate[..., e:e + 1] * _swiglu(h, w_gate[e], w_up[e], w_down[e])
    return y


def _rel_bucket(q_pos, k_pos):
    n = jnp.maximum(q_pos[:, None] - k_pos[None, :], 0)
    max_exact = N_BUCKETS // 2
    nf = jnp.maximum(n, 1).astype(jnp.float32)
    large = max_exact + (jnp.log(nf / max_exact) / math.log(MAX_DISTANCE / max_exact)
                         * (N_BUCKETS - max_exact)).astype(jnp.int32)
    large = jnp.minimum(large, N_BUCKETS - 1)
    return jnp.where(n < max_exact, n, large)


def _rel_bias(table, q_pos, k_pos):
    return jnp.transpose(table[_rel_bucket(q_pos, k_pos)], (2, 0, 1)).astype(jnp.float32)


def _conv_branch(u_hist, dw_w, dw_b, ln_g, ln_b, w_pw):
    y = lax.conv_general_dilated(u_hist, dw_w[:, None, :].astype(u_hist.dtype),
                                 window_strides=(1,), padding='VALID',
                                 dimension_numbers=('NWC', 'WIO', 'NWC'),
                                 feature_group_count=C_CONV)
    y = y + dw_b
    y = jax.nn.silu(_layernorm(y, ln_g, ln_b))
    return y @ w_pw


def _diff_attn_prompt(q, k, v, lam, rel_table):
    B, S = q.shape[0], q.shape[1]
    nqb = S // Q_BLOCK
    qb = q.reshape(B, nqb, Q_BLOCK, N_HEADS, 2 * HEAD_DIM).transpose(1, 0, 2, 3, 4)
    k1, k2 = k[..., :HEAD_DIM], k[..., HEAD_DIM:]
    k_pos = jnp.arange(S)
    scale = HEAD_DIM ** -0.5

    def block(args):
        qi, i = args
        q_pos = i * Q_BLOCK + jnp.arange(Q_BLOCK)
        bias = _rel_bias(rel_table, q_pos, k_pos)
        mask = k_pos[None, :] <= q_pos[:, None]

        def probs(qm, km):
            s = jnp.einsum('bqhd,bkhd->bhqk', qm, km).astype(jnp.float32) * scale + bias
            return jax.nn.softmax(jnp.where(mask, s, NEG_INF), axis=-1)

        a = probs(qi[..., :HEAD_DIM], k1) - lam * probs(qi[..., HEAD_DIM:], k2)
        return jnp.einsum('bhqk,bkhd->bqhd', a.astype(v.dtype), v)

    o = lax.map(block, (qb, jnp.arange(nqb)))
    return o.transpose(1, 0, 2, 3, 4).reshape(B, S, N_HEADS, V_DIM)


def _online_update(state, s, vb):
    m, l, acc = state
    m_new = jnp.maximum(m, jnp.max(s, axis=-1))
    p = jnp.exp(s - m_new[..., None])
    corr = jnp.exp(m - m_new)
    return (m_new, l * corr + jnp.sum(p, axis=-1),
            acc * corr[..., None] + jnp.einsum('bhqk,bkhd->bhqd', p, vb.astype(jnp.float32)))


def _diff_attn_sample(q, k_new, v_new, cache_k, cache_v, layer, page_table, lam, rel_table):
    DB, T = q.shape[0], q.shape[1]
    n_pages = page_table.shape[1]
    q_pos = n_pages * PAGE_SIZE + jnp.arange(T)
    q1, q2 = q[..., :HEAD_DIM], q[..., HEAD_DIM:]
    scale = HEAD_DIM ** -0.5

    def sc(qm, km, bias):
        return jnp.einsum('bqhd,bkhd->bhqk', qm, km).astype(jnp.float32) * scale + bias

    init = (jnp.full((DB, N_HEADS, T), NEG_INF, jnp.float32),
            jnp.zeros((DB, N_HEADS, T), jnp.float32),
            jnp.zeros((DB, N_HEADS, T, V_DIM), jnp.float32))

    def step(carry, xs):
        pt, p = xs
        kb = cache_k[layer, pt]
        vb = cache_v[layer, pt]
        bias = _rel_bias(rel_table, q_pos, p * PAGE_SIZE + jnp.arange(PAGE_SIZE))
        st1 = _online_update(carry[0], sc(q1, kb[..., :HEAD_DIM], bias), vb)
        st2 = _online_update(carry[1], sc(q2, kb[..., HEAD_DIM:], bias), vb)
        return (st1, st2), None

    (st1, st2), _ = lax.scan(step, (init, init), (page_table.T, jnp.arange(n_pages)))
    bias = _rel_bias(rel_table, q_pos, q_pos)
    mask = jnp.tril(jnp.ones((T, T), dtype=bool))
    s1 = jnp.where(mask, sc(q1, k_new[..., :HEAD_DIM], bias), NEG_INF)
    s2 = jnp.where(mask, sc(q2, k_new[..., HEAD_DIM:], bias), NEG_INF)
    st1 = _online_update(st1, s1, v_new)
    st2 = _online_update(st2, s2, v_new)
    o = st1[2] / st1[1][..., None] - lam * (st2[2] / st2[1][..., None])
    return o.transpose(0, 2, 1, 3).astype(q.dtype)


def _layer(x, hist, attn_fn, ffn_fn, w_in_l, g_mix, dw_w, dw_b, ln_g, ln_b, w_conv_o,
           sub_g, lam_init, w_attn_o, w_out_l, g_ffn):
    B, S = x.shape[0], x.shape[1]
    h = _rmsnorm(x, g_mix)
    z_glu, q, k, v, z_gc, z_ga = jnp.split(h @ w_in_l, IN_SPLITS, axis=-1)
    u = jax.nn.glu(z_glu, axis=-1)
    u_hist = jnp.concatenate([hist.astype(u.dtype), u], axis=1)
    y_conv = _conv_branch(u_hist, dw_w, dw_b, ln_g, ln_b, w_conv_o)
    q = q.reshape(B, S, N_HEADS, 2 * HEAD_DIM)
    k = k.reshape(B, S, N_HEADS, 2 * HEAD_DIM)
    v = v.reshape(B, S, N_HEADS, V_DIM)
    o = attn_fn(q, k, v)
    o = _rmsnorm(o, sub_g, SUBLN_EPS) * (1.0 - lam_init)
    y_attn = o.reshape(B, S, ATTN_WIDTH) @ w_attn_o
    mixed = jax.nn.sigmoid(z_gc) * y_conv + jax.nn.sigmoid(z_ga) * y_attn
    x = x + mixed @ w_out_l
    x = x + ffn_fn(_rmsnorm(x, g_ffn))
    return x, k, v, u_hist[:, -(CONV_WIDTH - 1):]


def setup_inputs(seed: int = 0) -> dict:
    key = jax.random.key(seed)
    ks = jax.random.split(key, 32)
    f32 = jnp.float32

    def nrm(k, shape, scale):
        return jax.random.normal(k, shape, f32) * scale

    n_pages = PAST_LEN // PAGE_SIZE
    n_used = DEC_BATCH * n_pages
    n_phys = n_used + (n_used + 3) // 4
    page_table = jax.random.permutation(ks[5], n_phys)[:n_used].reshape(DEC_BATCH, n_pages).astype(jnp.int32)
    return {
        'x_prompt': nrm(ks[0], (BATCH, SEQ, D_MODEL), 1.0),
        'x_sample': nrm(ks[1], (DEC_BATCH, DEC_SEQ, D_MODEL), 1.0),
        'cache_k': nrm(ks[2], (DEPTH, n_phys, PAGE_SIZE, N_HEADS, 2 * HEAD_DIM), 1.0),
        'cache_v': nrm(ks[3], (DEPTH, n_phys, PAGE_SIZE, N_HEADS, V_DIM), 1.0),
        'state_conv': nrm(ks[4], (DEPTH, DEC_BATCH, CONV_WIDTH - 1, C_CONV), 0.5),
        'page_table': page_table,
        'w_in': nrm(ks[6], (DEPTH, D_MODEL, D_IN), D_MODEL ** -0.5),
        'mix_norm_g': 1.0 + nrm(ks[7], (DEPTH, D_MODEL), 0.01),
        'conv_dw_w': nrm(ks[8], (DEPTH, CONV_WIDTH, C_CONV), CONV_WIDTH ** -0.5),
        'conv_dw_b': nrm(ks[9], (DEPTH, C_CONV), 0.01),
        'conv_ln_g': 1.0 + nrm(ks[10], (DEPTH, C_CONV), 0.01),
        'conv_ln_b': nrm(ks[11], (DEPTH, C_CONV), 0.01),
        'conv_w_out': nrm(ks[12], (DEPTH, C_CONV, D_MODEL), C_CONV ** -0.5),
        'lambda_q1': nrm(ks[13], (DEPTH, HEAD_DIM), 0.1),
        'lambda_k1': nrm(ks[14], (DEPTH, HEAD_DIM), 0.1),
        'lambda_q2': nrm(ks[15], (DEPTH, HEAD_DIM), 0.1),
        'lambda_k2': nrm(ks[16], (DEPTH, HEAD_DIM), 0.1),
        'subln_g': 1.0 + nrm(ks[17], (DEPTH, V_DIM), 0.01),
        'attn_w_out': nrm(ks[18], (DEPTH, ATTN_WIDTH, D_MODEL), ATTN_WIDTH ** -0.5),
        'rel_bias_table': nrm(ks[19], (N_BUCKETS, N_HEADS), 0.5),
        'w_out': nrm(ks[20], (DEPTH, D_MODEL, D_MODEL), D_MODEL ** -0.5),
        'ffn_norm_g': 1.0 + nrm(ks[21], (DEPTH, D_MODEL), 0.01),
        'ffn_w_gate': nrm(ks[22], (N_DENSE, D_MODEL, D_FF), D_MODEL ** -0.5),
        'ffn_w_up': nrm(ks[23], (N_DENSE, D_MODEL, D_FF), D_MODEL ** -0.5),
        'ffn_w_down': nrm(ks[24], (N_DENSE, D_FF, D_MODEL), D_FF ** -0.5),
        'moe_router': nrm(ks[25], (N_MOE, D_MODEL, N_EXPERTS), D_MODEL ** -0.5),
        'moe_w_gate': nrm(ks[26], (N_MOE, N_EXPERTS, D_MODEL, D_FF_EXPERT), D_MODEL ** -0.5),
        'moe_w_up': nrm(ks[27], (N_MOE, N_EXPERTS, D_MODEL, D_FF_EXPERT), D_MODEL ** -0.5),
        'moe_w_down': nrm(ks[28], (N_MOE, N_EXPERTS, D_FF_EXPERT, D_MODEL), D_FF_EXPERT ** -0.5),
        'final_norm_g': 1.0 + nrm(ks[29], (D_MODEL,), 0.01),
    }


def reference(x_prompt, x_sample, cache_k, cache_v, state_conv, page_table, w_in, mix_norm_g,
              conv_dw_w, conv_dw_b, conv_ln_g, conv_ln_b, conv_w_out, lambda_q1, lambda_k1,
              lambda_q2, lambda_k2, subln_g, attn_w_out, rel_bias_table, w_out, ffn_norm_g,
              ffn_w_gate, ffn_w_up, ffn_w_down, moe_router, moe_w_gate, moe_w_up, moe_w_down,
              final_norm_g):
    B, S = x_prompt.shape[0], x_prompt.shape[1]
    xp, xs = x_prompt, x_sample
    kp, vp, cp, ksm, vsm, csm = [], [], [], [], [], []
    for l in range(DEPTH):
        lam_init = 0.8 - 0.6 * math.exp(-0.3 * l)
        lam = (jnp.exp(jnp.sum(lambda_q1[l].astype(jnp.float32) * lambda_k1[l].astype(jnp.float32)))
               - jnp.exp(jnp.sum(lambda_q2[l].astype(jnp.float32) * lambda_k2[l].astype(jnp.float32)))
               + lam_init)
        if l % 2 == 0:
            i = l // 2
            ffn_fn = functools.partial(_swiglu, w_gate=ffn_w_gate[i], w_up=ffn_w_up[i],
                                       w_down=ffn_w_down[i])
        else:
            i = l // 2
            ffn_fn = functools.partial(_moe, router=moe_router[i], w_gate=moe_w_gate[i],
                                       w_up=moe_w_up[i], w_down=moe_w_down[i])
        params = (w_in[l], mix_norm_g[l], conv_dw_w[l], conv_dw_b[l], conv_ln_g[l], conv_ln_b[l],
                  conv_w_out[l], subln_g[l], lam_init, attn_w_out[l], w_out[l], ffn_norm_g[l])
        attn_p = functools.partial(_diff_attn_prompt, lam=lam, rel_table=rel_bias_table)
        hist_p = jnp.zeros((B, CONV_WIDTH - 1, C_CONV), xp.dtype)
        xp, k_p, v_p, c_p = _layer(xp, hist_p, attn_p, ffn_fn, *params)
        kp.append(k_p.reshape(B * S // PAGE_SIZE, PAGE_SIZE, N_HEADS, 2 * HEAD_DIM))
        vp.append(v_p.reshape(B * S // PAGE_SIZE, PAGE_SIZE, N_HEADS, V_DIM))
        cp.append(c_p)
        attn_s = functools.partial(_diff_attn_sample, cache_k=cache_k, cache_v=cache_v, layer=l,
                                   page_table=page_table, lam=lam, rel_table=rel_bias_table)
        xs, k_s, v_s, c_s = _layer(xs, state_conv[l], attn_s, ffn_fn, *params)
        ksm.append(k_s)
        vsm.append(v_s)
        csm.append(c_s)
    y_prompt = _rmsnorm(xp, final_norm_g)
    y_sample = _rmsnorm(xs, final_norm_g)
    new_k_prompt = jnp.stack(kp)
    new_v_prompt = jnp.stack(vp)
    new_conv_prompt = jnp.stack(cp)
    new_k_sample = jnp.stack(ksm)
    new_v_sample = jnp.stack(vsm)
    new_conv_sample = jnp.stack(csm)
    return (y_prompt, y_sample, new_k_prompt, new_v_prompt, new_conv_prompt,
            new_k_sample, new_v_sample, new_conv_sample)
```

```python
import functools
import math

import numpy as np
import jax
import jax.numpy as jnp
from jax import lax
from jax.experimental import pallas as pl
from jax.experimental.pallas import tpu as pltpu

F32 = jnp.float32
BF16 = jnp.bfloat16

D_MODEL = 1024
N_HEADS = 8
HEAD_DIM = 64
V_DIM = 2 * HEAD_DIM
N_SEG = 7
CONV_WIDTH = 31
HALO = 32
N_BUCKETS = 32
MAX_DISTANCE = 128
PAGE_SIZE = 128
N_EXPERTS = 8
RMS_EPS = 1e-6
SUBLN_EPS = 1e-5
LN_EPS = 1e-5
NEG_INF = -1e30
ATTN_GROUP = 4
LANES = 128
SUBLANES = 8
VMEM_LIMIT = 56 * 1024 * 1024


def _params(*sem):
    return pltpu.CompilerParams(dimension_semantics=sem, vmem_limit_bytes=VMEM_LIMIT)


def _row_tile(m, pref=512):
    return pref if m % pref == 0 else m


def _resident(shape, index_map):
    return pl.BlockSpec(shape, index_map, pipeline_mode=pl.Buffered(1))


def _rms(x, g, eps):
    return x * lax.rsqrt(jnp.mean(x * x, axis=-1, keepdims=True) + eps) * g


def _bias_body(ids_ref, texp_ref, o_ref):
    ids = ids_ref[0]
    acc = jnp.where(ids < 0, NEG_INF, 0.0).astype(F32)
    for n in range(N_BUCKETS):
        acc = jnp.where(ids == n, texp_ref[0, n:n + 1, :], acc)
    o_ref[0, 0] = acc


def _bias_tiles(ids, texp):
    k, r, c = ids.shape
    g = texp.shape[0]
    return pl.pallas_call(
        _bias_body,
        out_shape=jax.ShapeDtypeStruct((g, k, r, c), F32),
        grid=(g, k),
        in_specs=[pl.BlockSpec((1, r, c), lambda a, b: (b, 0, 0)),
                  pl.BlockSpec((1, N_BUCKETS, c), lambda a, b: (a, 0, 0))],
        out_specs=pl.BlockSpec((1, 1, r, c), lambda a, b: (a, b, 0, 0)),
        compiler_params=_params("arbitrary", "arbitrary"),
        name="bias_tiles",
    )(ids, texp)


def _bucket_np(n):
    n = np.maximum(n, 0)
    max_exact = N_BUCKETS // 2
    nf = np.maximum(n, 1).astype(np.float32)
    large = max_exact + (np.log(nf / max_exact) / math.log(MAX_DISTANCE / max_exact)
                         * (N_BUCKETS - max_exact)).astype(np.int32)
    large = np.minimum(large, N_BUCKETS - 1)
    return np.where(n < max_exact, n, large).astype(np.int32)


def _inproj_body(x_ref, g_ref, w_ref, u_ref, q_ref, k_ref, v_ref, kb_ref, vb_ref, gc_ref, ga_ref):
    h = _rms(x_ref[...], g_ref[...], RMS_EPS).astype(BF16)

    def seg(c):
        return jnp.dot(h, w_ref[0, :, c * D_MODEL:(c + 1) * D_MODEL], preferred_element_type=F32)

    u_ref[...] = seg(0) * jax.nn.sigmoid(seg(1))
    q_ref[...] = (seg(2) * HEAD_DIM ** -0.5).astype(BF16)
    k = seg(3)
    k_ref[...] = k
    kb_ref[...] = k.astype(BF16)
    v = seg(4)
    v_ref[...] = v
    vb_ref[...] = v.astype(BF16)
    gc_ref[...] = jax.nn.sigmoid(seg(5))
    ga_ref[...] = jax.nn.sigmoid(seg(6))


def _inproj(x, g, w_all, layer):
    m = x.shape[0]
    tm = _row_tile(m, 256)
    row = lambda i: (i, 0)
    blk = pl.BlockSpec((tm, D_MODEL), row)
    sd = lambda dt: jax.ShapeDtypeStruct((m, D_MODEL), dt)
    return pl.pallas_call(
        _inproj_body,
        out_shape=(sd(F32), sd(BF16), sd(F32), sd(F32), sd(BF16), sd(BF16), sd(F32), sd(F32)),
        grid=(m // tm,),
        in_specs=[blk,
                  _resident((1, D_MODEL), lambda i: (0, 0)),
                  _resident((1, D_MODEL, N_SEG * D_MODEL), lambda i: (layer, 0, 0))],
        out_specs=(blk,) * 8,
        compiler_params=_params("parallel"),
        name="inproj",
    )(x, g, w_all)


def _conv_body(u_ref, halo_ref, hist_ref, dw_ref, db_ref, lg_ref, lb_ref, wpw_ref, gc_ref,
               o_ref, win_ref, cv_ref, *, bb, tm, rc):
    first = pl.program_id(1) == 0
    rnd = lambda a: a.astype(BF16).astype(F32)
    win_ref[:, HALO:HALO + tm, :] = rnd(u_ref[...])
    win_ref[:, HALO + tm:, :] = jnp.zeros((bb, SUBLANES, D_MODEL), F32)

    @pl.when(first)
    def _():
        win_ref[:, 0:HALO, :] = rnd(hist_ref[...])

    @pl.when(jnp.logical_not(first))
    def _():
        win_ref[:, 0:HALO, :] = rnd(halo_ref[...])

    n_chunks = tm // rc
    span = rc + HALO + SUBLANES

    def chunk(idx, carry):
        b = idx // n_chunks
        r0 = pl.multiple_of((idx % n_chunks) * rc, SUBLANES)
        for lc in range(D_MODEL // LANES):
            ls = slice(lc * LANES, (lc + 1) * LANES)
            x = win_ref[b, pl.ds(r0, span), ls]
            acc = jnp.zeros((rc, LANES), F32)
            for s in range(SUBLANES):
                xs = x if s == 0 else pltpu.roll(x, span - s, axis=0)
                for a in range(HALO // SUBLANES + 1):
                    j = SUBLANES * a + s - (HALO - CONV_WIDTH + 1)
                    if 0 <= j < CONV_WIDTH:
                        acc = acc + rnd(dw_ref[0, j:j + 1, ls]) * xs[SUBLANES * a:SUBLANES * a + rc]
            cv_ref[pl.ds(pl.multiple_of(b * tm + r0, SUBLANES), rc), ls] = acc
        return carry

    lax.fori_loop(0, bb * n_chunks, chunk, 0)

    y = cv_ref[...] + db_ref[...]
    mu = jnp.mean(y, axis=-1, keepdims=True)
    yc = y - mu
    var = jnp.mean(yc * yc, axis=-1, keepdims=True)
    y = yc * lax.rsqrt(var + LN_EPS) * lg_ref[...] + lb_ref[...]
    y = (y * jax.nn.sigmoid(y)).astype(BF16)
    z = jnp.dot(y, wpw_ref[0], preferred_element_type=F32)
    o_ref[...] = (gc_ref[...].reshape(bb * tm, D_MODEL) * z).reshape(bb, tm, D_MODEL)


def _conv(u, hist, dw, db, lg, lb, wpw_all, gc, layer, *, bb, tm):
    nb, s, c = u.shape
    rc = min(tm, 64)
    n_t = s // tm
    if n_t > 1:
        halo_src = u
        per = tm // HALO
        halo_map = lambda b, i: (b, jnp.maximum(i * per - 1, 0), 0)
    else:
        halo_src = hist
        halo_map = lambda b, i: (b, 0, 0)
    vec = lambda: _resident((1, 1, c), lambda b, i: (layer, 0, 0))
    return pl.pallas_call(
        functools.partial(_conv_body, bb=bb, tm=tm, rc=rc),
        out_shape=jax.ShapeDtypeStruct((nb, s, c), F32),
        grid=(nb // bb, n_t),
        in_specs=[pl.BlockSpec((bb, tm, c), lambda b, i: (b, i, 0)),
                  pl.BlockSpec((bb, HALO, c), halo_map),
                  pl.BlockSpec((bb, HALO, c), lambda b, i: (b, 0, 0)),
                  _resident((1, CONV_WIDTH, c), lambda b, i: (layer, 0, 0)),
                  vec(), vec(), vec(),
                  _resident((1, c, D_MODEL), lambda b, i: (layer, 0, 0)),
                  pl.BlockSpec((bb, tm, c), lambda b, i: (b, i, 0))],
        out_specs=pl.BlockSpec((bb, tm, c), lambda b, i: (b, i, 0)),
        scratch_shapes=[pltpu.VMEM((bb, tm + HALO + SUBLANES, c), F32),
                        pltpu.VMEM((bb * tm, c), F32)],
        compiler_params=_params("parallel", "arbitrary"),
        name="conv",
    )(u, halo_src, hist, dw, db, lg, lb, wpw_all, gc)


def _lambda(lam_ref, lam_init):
    t = lam_ref[0]
    a = jnp.sum(t[0:1] * t[1:2], axis=-1, keepdims=True)
    b = jnp.sum(t[2:3] * t[3:4], axis=-1, keepdims=True)
    return jnp.exp(a) - jnp.exp(b) + lam_init


def _subln(o, sg, lam_init):
    return _rms(o, sg, SUBLN_EPS) * (1.0 - lam_init)


def _attn_prompt_body(tbl_ref, lam_ref, sg_ref, q_ref, k_ref, v_ref, bias_ref, o_ref,
                      s_ref, red_ref, m_ref, off_ref, c_ref, acc_ref, *, t, lam_init):
    h = pl.program_id(1)
    i = pl.program_id(2)
    q = q_ref[0]
    lane = lax.broadcasted_iota(jnp.int32, (t, V_DIM), 1)
    zero = jnp.zeros((t, V_DIM), BF16)
    qs = jnp.concatenate([jnp.where(lane < HEAD_DIM, q, zero), jnp.where(lane >= HEAD_DIM, q, zero)], axis=0)
    far = tbl_ref[N_BUCKETS - 1, h]
    nl = t // LANES
    wide = (2 * t, LANES)

    def scores(j, n=1):
        k = k_ref[0, pl.ds(pl.multiple_of(j * t, t), n * t), :]
        return lax.dot_general(qs, k, (((1,), (1,)), ((), ())), preferred_element_type=F32)

    def fold(x, op):
        r = x[:, 0:LANES]
        for c in range(1, x.shape[1] // LANES):
            r = op(r, x[:, c * LANES:(c + 1) * LANES])
        return r

    def grouped(n_tiles, body):
        n_groups = n_tiles // ATTN_GROUP

        def group_step(g, carry):
            body(g * ATTN_GROUP, ATTN_GROUP)
            return carry

        def single_step(j, carry):
            body(j, 1)
            return carry

        lax.fori_loop(0, n_groups, group_step, 0)
        lax.fori_loop(n_groups * ATTN_GROUP, n_tiles, single_step, 0)

    def rep(x):
        return jnp.concatenate([x] * nl, axis=1)

    red_ref[...] = jnp.full(wide, NEG_INF, F32)

    n_far = jnp.maximum(i - 1, 0)

    def far_a(j, n):
        s = scores(j, n)
        for c in range(n):
            s_ref[j + c] = s[:, c * t:(c + 1) * t]
        red_ref[...] = jnp.maximum(red_ref[...], fold(s, jnp.maximum))

    grouped(n_far, far_a)
    m_ref[...] = jnp.broadcast_to(jnp.max(red_ref[...], axis=-1, keepdims=True) + far, wide)

    def near_a(j, which):
        bias = bias_ref[0, which]
        s = scores(j) + jnp.concatenate([bias, bias], axis=0)
        s_ref[j] = s
        m_ref[...] = jnp.maximum(m_ref[...], jnp.broadcast_to(jnp.max(s, axis=-1, keepdims=True), wide))

    @pl.when(i >= 1)
    def _():
        near_a(i - 1, 1)

    near_a(i, 0)

    red_ref[...] = jnp.zeros(wide, F32)
    off_ref[...] = m_ref[...] - far

    def expo(j, off):
        e = jnp.exp(s_ref[j] - rep(off))
        s_ref[j] = e
        red_ref[...] = red_ref[...] + fold(e, jnp.add)

    def far_b(j, n):
        for c in range(n):
            expo(j + c, off_ref[...])

    grouped(n_far, far_b)

    @pl.when(i >= 1)
    def _():
        expo(i - 1, m_ref[...])

    expo(i, m_ref[...])

    inv = 1.0 / jnp.sum(red_ref[...], axis=-1, keepdims=True)
    lam = _lambda(lam_ref, lam_init)
    c_ref[0:t, :] = jnp.broadcast_to(inv[0:t], (t, LANES))
    c_ref[t:2 * t, :] = jnp.broadcast_to(lam * inv[t:2 * t], (t, LANES))

    acc_ref[...] = jnp.zeros((t, V_DIM), F32)

    def pv(j, n):
        c1, c2 = rep(c_ref[0:t, :]), rep(c_ref[t:2 * t, :])
        a = [(s_ref[j + c, 0:t, :] * c1 - s_ref[j + c, t:2 * t, :] * c2).astype(BF16) for c in range(n)]
        v = v_ref[0, pl.ds(pl.multiple_of(j * t, t), n * t), :]
        acc_ref[...] += jnp.dot(jnp.concatenate(a, axis=1), v, preferred_element_type=F32)

    grouped(i + 1, pv)
    o_ref[0] = _subln(acc_ref[...], sg_ref[0], lam_init).astype(BF16)


def _attn_prompt(q, k, v, bias, table, lam_all, sg_all, layer, lam_init, t):
    b, s, _ = q.shape
    smem = pl.BlockSpec(memory_space=pltpu.SMEM)
    wide = (2 * t, LANES)
    return pl.pallas_call(
        functools.partial(_attn_prompt_body, t=t, lam_init=lam_init),
        out_shape=jax.ShapeDtypeStruct((b, s, N_HEADS * V_DIM), BF16),
        grid=(b, N_HEADS, s // t),
        in_specs=[smem,
                  pl.BlockSpec((1, 4, HEAD_DIM), lambda bi, h, i: (layer, 0, 0)),
                  pl.BlockSpec((1, 1, V_DIM), lambda bi, h, i: (layer, 0, 0)),
                  pl.BlockSpec((1, t, V_DIM), lambda bi, h, i: (bi, i, h)),
                  pl.BlockSpec((1, s, V_DIM), lambda bi, h, i: (bi, 0, h)),
                  pl.BlockSpec((1, s, V_DIM), lambda bi, h, i: (bi, 0, h)),
                  pl.BlockSpec((1, 2, t, t), lambda bi, h, i: (h, 0, 0, 0))],
        out_specs=pl.BlockSpec((1, t, V_DIM), lambda bi, h, i: (bi, i, h)),
        scratch_shapes=[pltpu.VMEM((s // t, 2 * t, t), F32),
                        pltpu.VMEM(wide, F32), pltpu.VMEM(wide, F32), pltpu.VMEM(wide, F32), pltpu.VMEM(wide, F32),
                        pltpu.VMEM((t, V_DIM), F32)],
        compiler_params=_params("parallel", "parallel", "arbitrary"),
        name="attn_prompt",
    )(table, lam_all, sg_all, q, k, v, bias)


N_COLS = N_HEADS * 2 * 8


def _attn_sample_body(pt_ref, lam_ref, sg_ref, qbd_ref, bias_ref, kn_ref, vn_ref, *rest,
                      g, n_steps, lam_init):
    k_refs = rest[:g]
    v_refs = rest[g:2 * g]
    o_ref, m_ref, l_ref, a_ref = rest[2 * g:]
    step = pl.program_id(1)

    @pl.when(step == 0)
    def _():
        m_ref[...] = jnp.full((1, N_COLS), NEG_INF, F32)
        l_ref[...] = jnp.zeros((1, N_COLS), F32)
        a_ref[...] = jnp.zeros((N_COLS, V_DIM), F32)

    qbd = qbd_ref[0]

    def wide(ref):
        return jnp.concatenate(
            [ref[0, 0, pl.ds(hh, PAGE_SIZE, stride=N_HEADS), :] for hh in range(N_HEADS)], axis=1)

    def scores(kw, bias):
        return jnp.dot(kw, qbd, preferred_element_type=F32) + bias

    def accumulate(s_list, v_list):
        m_prev = m_ref[...]
        m_new = m_prev
        for s in s_list:
            m_new = jnp.maximum(m_new, jnp.max(s, axis=0, keepdims=True))
        alpha = jnp.exp(m_prev - m_new)
        l_new = alpha * l_ref[...]
        upd = jnp.zeros((N_COLS, V_DIM), F32)
        for s, vw in zip(s_list, v_list):
            p = jnp.exp(s - m_new)
            l_new = l_new + jnp.sum(p, axis=0, keepdims=True)
            full = jnp.dot(p.T, vw, preferred_element_type=F32)
            upd = upd + jnp.concatenate(
                [full[hh * 16:(hh + 1) * 16, hh * V_DIM:(hh + 1) * V_DIM] for hh in range(N_HEADS)], axis=0)
        alpha_col = jnp.broadcast_to(alpha, (N_COLS, N_COLS)).T
        a_ref[...] = alpha_col * a_ref[...] + upd
        l_ref[...] = l_new
        m_ref[...] = m_new

    far = bias_ref[0, 0, 0:1, :]
    near = bias_ref[0, 0, SUBLANES:SUBLANES + PAGE_SIZE, :]
    self_bias = bias_ref[0, 0, SUBLANES + PAGE_SIZE:, :]

    @pl.when(step < n_steps - 1)
    def _():
        accumulate([scores(wide(r), far) for r in k_refs], [wide(r) for r in v_refs])

    @pl.when(step == n_steps - 1)
    def _():
        s_list = [scores(wide(r), far) for r in k_refs[:-1]]
        s_list.append(scores(wide(k_refs[-1]), near))
        accumulate(s_list, [wide(r) for r in v_refs])
        pad = jnp.zeros((PAGE_SIZE - 8, N_HEADS * V_DIM), F32)
        kn = jnp.concatenate([kn_ref[0], pad], axis=0)
        vn = jnp.concatenate([vn_ref[0], pad], axis=0)
        accumulate([scores(kn, self_bias)], [vn])
        lam = _lambda(lam_ref, lam_init)
        inv_l = jnp.broadcast_to(1.0 / l_ref[...], (N_COLS, N_COLS)).T
        acc = a_ref[...] * inv_l
        for hh in range(N_HEADS):
            o = acc[hh * 16:hh * 16 + 8] - lam * acc[hh * 16 + 8:hh * 16 + 16]
            o_ref[0, :, hh * V_DIM:(hh + 1) * V_DIM] = _subln(o, sg_ref[0], lam_init).astype(BF16)


def _attn_sample(q, k_new, v_new, cache_k, cache_v, page_table, bias, lam_all, sg_all, layer, lam_init):
    db, tq, _ = q.shape
    n_pages = page_table.shape[1]
    g = 8 if n_pages % 8 == 0 else n_pages
    n_steps = n_pages // g
    q5 = q.astype(F32).reshape(db, tq, N_HEADS, V_DIM).transpose(0, 2, 3, 1)
    half = (jnp.arange(V_DIM)[:, None] >= HEAD_DIM) == (jnp.arange(2)[None, :] == 1)
    qm = q5[:, :, :, None, :] * half.astype(F32)[None, None, :, :, None]
    eye = jnp.eye(N_HEADS, dtype=F32)
    qbd = (qm[:, :, :, None, :, :] * eye[None, :, None, :, None, None]).reshape(db, N_HEADS * V_DIM, N_COLS)

    def page_map(j):
        return lambda b, s, pt: (layer, pt[b * n_pages + s * g + j], 0, 0)

    page_blk = (1, 1, PAGE_SIZE * N_HEADS, V_DIM)
    in_specs = [pl.BlockSpec((1, 4, HEAD_DIM), lambda b, s, pt: (layer, 0, 0)),
                pl.BlockSpec((1, 1, V_DIM), lambda b, s, pt: (layer, 0, 0)),
                pl.BlockSpec((1, N_HEADS * V_DIM, N_COLS), lambda b, s, pt: (b, 0, 0)),
                pl.BlockSpec((1, 1) + bias.shape[2:], lambda b, s, pt: (0, 0, 0, 0)),
                pl.BlockSpec((1, tq, N_HEADS * V_DIM), lambda b, s, pt: (b, 0, 0)),
                pl.BlockSpec((1, tq, N_HEADS * V_DIM), lambda b, s, pt: (b, 0, 0))]
    in_specs += [pl.BlockSpec(page_blk, page_map(j)) for j in range(g)]
    in_specs += [pl.BlockSpec(page_blk, page_map(j)) for j in range(g)]
    return pl.pallas_call(
        functools.partial(_attn_sample_body, g=g, n_steps=n_steps, lam_init=lam_init),
        out_shape=jax.ShapeDtypeStruct((db, tq, N_HEADS * V_DIM), BF16),
        grid_spec=pltpu.PrefetchScalarGridSpec(
            num_scalar_prefetch=1,
            grid=(db, n_steps),
            in_specs=in_specs,
            out_specs=pl.BlockSpec((1, tq, N_HEADS * V_DIM), lambda b, s, pt: (b, 0, 0)),
            scratch_shapes=[pltpu.VMEM((1, N_COLS), F32), pltpu.VMEM((1, N_COLS), F32),
                            pltpu.VMEM((N_COLS, V_DIM), F32)]),
        compiler_params=_params("parallel", "arbitrary"),
        name="attn_sample",
    )(page_table.reshape(-1), lam_all, sg_all, qbd, bias, k_new, v_new,
      *([cache_k] * g), *([cache_v] * g))


def _top2_gate(logits):
    lane = lax.broadcasted_iota(jnp.int32, logits.shape, 1)
    lg = jnp.where(lane < N_EXPERTS, logits, -jnp.inf)
    m1 = jnp.max(lg, axis=-1, keepdims=True)
    i1 = jnp.min(jnp.where(lg == m1, lane, LANES), axis=-1, keepdims=True)
    lg2 = jnp.where(lane == i1, -jnp.inf, lg)
    m2 = jnp.max(lg2, axis=-1, keepdims=True)
    i2 = jnp.min(jnp.where(lg2 == m2, lane, LANES), axis=-1, keepdims=True)
    e2 = jnp.exp(m2 - m1)
    w1 = 1.0 / (1.0 + e2)
    w2 = e2 / (1.0 + e2)
    return jnp.where(lane == i1, w1, 0.0) + jnp.where(lane == i2, w2, 0.0)


def _merge_body(x_ref, yc_ref, o_ref, ga_ref, wao_ref, wout_ref, gf_ref, *rest, routed):
    if routed:
        wr_ref, xn_ref, h_ref, gate_ref = rest
    else:
        xn_ref, h_ref = rest
    ya = jnp.dot(o_ref[...], wao_ref[0], preferred_element_type=F32)
    mixed = yc_ref[...] + ga_ref[...] * ya
    xn = x_ref[...] + jnp.dot(mixed.astype(BF16), wout_ref[0], preferred_element_type=F32)
    xn_ref[...] = xn
    h = _rms(xn, gf_ref[0], RMS_EPS)
    hb = h.astype(BF16)
    h_ref[...] = hb
    if routed:
        gate_ref[...] = _top2_gate(jnp.dot(hb, wr_ref[0], preferred_element_type=F32))


def _merge(x, yc, o, ga, wao_all, wout_all, gf_all, layer, router_all=None, ridx=0):
    m = x.shape[0]
    tm = _row_tile(m)
    routed = router_all is not None
    blk = pl.BlockSpec((tm, D_MODEL), lambda i: (i, 0))
    wspec = lambda: _resident((1, D_MODEL, D_MODEL), lambda i: (layer, 0, 0))
    in_specs = [blk, blk, blk, blk, wspec(), wspec(), _resident((1, 1, D_MODEL), lambda i: (layer, 0, 0))]
    args = [x, yc, o, ga, wao_all, wout_all, gf_all]
    out_shape = [jax.ShapeDtypeStruct((m, D_MODEL), F32), jax.ShapeDtypeStruct((m, D_MODEL), BF16)]
    out_specs = [blk, blk]
    if routed:
        in_specs.append(_resident((1, D_MODEL, LANES), lambda i: (ridx, 0, 0)))
        args.append(router_all)
        out_shape.append(jax.ShapeDtypeStruct((m, LANES), F32))
        out_specs.append(pl.BlockSpec((tm, LANES), lambda i: (i, 0)))
    return pl.pallas_call(
        functools.partial(_merge_body, routed=routed),
        out_shape=tuple(out_shape),
        grid=(m // tm,),
        in_specs=in_specs,
        out_specs=tuple(out_specs),
        compiler_params=_params("parallel"),
        name="merge",
    )(*args)


def _ffn_body(h_ref, x_ref, wg_ref, wu_ref, wd_ref, o_ref):
    h = h_ref[...]
    a = jnp.dot(h, wg_ref[0], preferred_element_type=F32)
    b = jnp.dot(h, wu_ref[0], preferred_element_type=F32)
    hid = (a * jax.nn.sigmoid(a) * b).astype(BF16)
    o_ref[...] = x_ref[...] + jnp.dot(hid, wd_ref[0], preferred_element_type=F32)


def _ffn(h, x, wg_all, wu_all, wd_all, idx):
    m = x.shape[0]
    tm = _row_tile(m)
    f = wg_all.shape[2]
    blk = pl.BlockSpec((tm, D_MODEL), lambda i: (i, 0))
    return pl.pallas_call(
        _ffn_body,
        out_shape=jax.ShapeDtypeStruct((m, D_MODEL), F32),
        grid=(m // tm,),
        in_specs=[blk, blk,
                  _resident((1, D_MODEL, f), lambda i: (idx, 0, 0)),
                  _resident((1, D_MODEL, f), lambda i: (idx, 0, 0)),
                  _resident((1, f, D_MODEL), lambda i: (idx, 0, 0))],
        out_specs=blk,
        compiler_params=_params("parallel"),
        name="ffn",
    )(h, x, wg_all, wu_all, wd_all)


def _moe_body(h_ref, x_ref, gate_ref, wg_ref, wu_ref, wd_ref, o_ref, acc_ref, *, n_f):
    e = pl.program_id(1)
    f = pl.program_id(2)

    @pl.when(jnp.logical_and(e == 0, f == 0))
    def _():
        acc_ref[...] = jnp.zeros(acc_ref.shape, F32)

    gate = gate_ref[...]
    lane = lax.broadcasted_iota(jnp.int32, gate.shape, 1)
    ge = jnp.sum(jnp.where(lane == e, gate, 0.0), axis=-1, keepdims=True)
    h = h_ref[...]
    a = jnp.dot(h, wg_ref[0, 0], preferred_element_type=F32)
    b = jnp.dot(h, wu_ref[0, 0], preferred_element_type=F32)
    hid = (a * jax.nn.sigmoid(a) * b).astype(BF16)
    acc_ref[...] += ge * jnp.dot(hid, wd_ref[0, 0], preferred_element_type=F32)

    @pl.when(jnp.logical_and(e == N_EXPERTS - 1, f == n_f - 1))
    def _():
        o_ref[...] = x_ref[...] + acc_ref[...]


def _moe(h, x, gate, wg_all, wu_all, wd_all, idx):
    m = x.shape[0]
    tm = _row_tile(m)
    fe = wg_all.shape[3]
    tf = fe // 2
    n_f = fe // tf
    blk = pl.BlockSpec((tm, D_MODEL), lambda i, e, f: (i, 0))
    return pl.pallas_call(
        functools.partial(_moe_body, n_f=n_f),
        out_shape=jax.ShapeDtypeStruct((m, D_MODEL), F32),
        grid=(m // tm, N_EXPERTS, n_f),
        in_specs=[blk, blk,
                  pl.BlockSpec((tm, LANES), lambda i, e, f: (i, 0)),
                  pl.BlockSpec((1, 1, D_MODEL, tf), lambda i, e, f: (idx, e, 0, f)),
                  pl.BlockSpec((1, 1, D_MODEL, tf), lambda i, e, f: (idx, e, 0, f)),
                  pl.BlockSpec((1, 1, tf, D_MODEL), lambda i, e, f: (idx, e, f, 0))],
        out_specs=blk,
        scratch_shapes=[pltpu.VMEM((tm, D_MODEL), F32)],
        compiler_params=_params("parallel", "arbitrary", "arbitrary"),
        name="moe",
    )(h, x, gate, wg_all, wu_all, wd_all)


def _final_body(x_ref, g_ref, o_ref):
    o_ref[...] = _rms(x_ref[...], g_ref[...], RMS_EPS)


def _final_norm(x, g):
    m = x.shape[0]
    tm = _row_tile(m)
    blk = pl.BlockSpec((tm, D_MODEL), lambda i: (i, 0))
    return pl.pallas_call(
        _final_body,
        out_shape=jax.ShapeDtypeStruct((m, D_MODEL), F32),
        grid=(m // tm,),
        in_specs=[blk, _resident((1, D_MODEL), lambda i: (0, 0))],
        out_specs=blk,
        compiler_params=_params("parallel"),
        name="final_norm",
    )(x, g)


def _prompt_bias_ids(t):
    qq = np.arange(t)[:, None]
    kk = np.arange(t)[None, :]
    diag = np.where(kk <= qq, _bucket_np(qq - kk), -1)
    below = _bucket_np(t + qq - kk)
    return np.stack([diag, below]).astype(np.int32)


def _sample_bias_ids(past_len, tq):
    tcol = np.tile(np.arange(tq), N_HEADS * 2)[None, :]
    far = np.full((SUBLANES, N_COLS), N_BUCKETS - 1)
    kpos = past_len - PAGE_SIZE + np.arange(PAGE_SIZE)[:, None]
    near = _bucket_np(past_len + tcol - kpos)
    tk = np.arange(PAGE_SIZE)[:, None]
    self_ids = np.where((tk < tq) & (tk <= tcol), _bucket_np(tcol - tk), -1)
    return np.concatenate([far, near, self_ids])[None].astype(np.int32)


def kernel(x_prompt, x_sample, cache_k, cache_v, state_conv, page_table, w_in, mix_norm_g, conv_dw_w, conv_dw_b, conv_ln_g, conv_ln_b, conv_w_out, lambda_q1, lambda_k1, lambda_q2, lambda_k2, subln_g, attn_w_out, rel_bias_table, w_out, ffn_norm_g, ffn_w_gate, ffn_w_up, ffn_w_down, moe_router, moe_w_gate, moe_w_up, moe_w_down, final_norm_g):
    b, s, _ = x_prompt.shape
    db, tq, _ = x_sample.shape
    depth = w_in.shape[0]
    n_phys = cache_k.shape[1]
    n_pages = page_table.shape[1]
    past_len = n_pages * PAGE_SIZE
    t_attn = min(256, s)
    assert s % t_attn == 0 and t_attn >= PAGE_SIZE and tq == SUBLANES
    assert (_bucket_np(np.arange(min(t_attn, PAGE_SIZE) + 1, max(s, past_len) + tq + 1)) == N_BUCKETS - 1).all()

    w_in_b = w_in.astype(BF16)
    conv_w_out_b = conv_w_out.astype(BF16)
    attn_w_out_b = attn_w_out.astype(BF16)
    w_out_b = w_out.astype(BF16)
    ffn_wg, ffn_wu, ffn_wd = (w.astype(BF16) for w in (ffn_w_gate, ffn_w_up, ffn_w_down))
    moe_wg, moe_wu, moe_wd = (w.astype(BF16) for w in (moe_w_gate, moe_w_up, moe_w_down))
    router_p = jnp.pad(moe_router, ((0, 0), (0, 0), (0, LANES - N_EXPERTS))).astype(BF16)
    lam_all = jnp.stack([lambda_q1, lambda_k1, lambda_q2, lambda_k2], axis=1)
    vec3 = lambda a: a.reshape(depth, 1, -1)
    sg_all, gf_all = vec3(subln_g), vec3(ffn_norm_g)
    db_all, lg_all, lb_all = vec3(conv_dw_b), vec3(conv_ln_g), vec3(conv_ln_b)
    cache_k4 = cache_k.reshape(depth, n_phys, PAGE_SIZE * N_HEADS, V_DIM)
    cache_v4 = cache_v.reshape(depth, n_phys, PAGE_SIZE * N_HEADS, V_DIM)

    texp_p = jnp.broadcast_to(rel_bias_table.T[:, :, None], (N_HEADS, N_BUCKETS, t_attn))
    bias_p = _bias_tiles(jnp.asarray(_prompt_bias_ids(t_attn)), texp_p)
    texp_s = jnp.repeat(rel_bias_table, N_COLS // N_HEADS, axis=1)[None]
    bias_s = _bias_tiles(jnp.asarray(_sample_bias_ids(past_len, tq)), texp_s)

    hist_p = jnp.zeros((b, HALO, D_MODEL), F32)
    xp = x_prompt.reshape(b * s, D_MODEL)
    xs = x_sample.reshape(db * tq, D_MODEL)
    outs = [[] for _ in range(6)]
    for l in range(depth):
        lam_init = 0.8 - 0.6 * math.exp(-0.3 * l)
        g_mix = mix_norm_g[l].reshape(1, D_MODEL)
        new_x = []
        for grp, x in enumerate((xp, xs)):
            u, q, k, v, kb, vb, gc, ga = _inproj(x, g_mix, w_in_b, l)
            if grp == 0:
                r3 = lambda a: a.reshape(b, s, D_MODEL)
                yc = _conv(r3(u), hist_p, conv_dw_w, db_all, lg_all, lb_all, conv_w_out_b, r3(gc), l,
                           bb=1, tm=min(512, s))
                o = _attn_prompt(r3(q), r3(kb), r3(vb), bias_p, rel_bias_table, lam_all, sg_all, l, lam_init, t_attn)
                n_pg = b * s // PAGE_SIZE
                outs[0].append(k.reshape(n_pg, PAGE_SIZE, N_HEADS, V_DIM))
                outs[1].append(v.reshape(n_pg, PAGE_SIZE, N_HEADS, V_DIM))
                outs[2].append(r3(u)[:, s - (CONV_WIDTH - 1):, :])
            else:
                r3 = lambda a: a.reshape(db, tq, D_MODEL)
                hist_s = jnp.concatenate(
                    [jnp.zeros((db, HALO - CONV_WIDTH + 1, D_MODEL), F32), state_conv[l]], axis=1)
                yc = _conv(r3(u), hist_s, conv_dw_w, db_all, lg_all, lb_all, conv_w_out_b, r3(gc), l,
                           bb=min(8, db), tm=tq)
                o = _attn_sample(r3(q), r3(k), r3(v), cache_k4, cache_v4, page_table, bias_s,
                                 lam_all, sg_all, l, lam_init)
                outs[3].append(k.reshape(db, tq, N_HEADS, V_DIM))
                outs[4].append(v.reshape(db, tq, N_HEADS, V_DIM))
                outs[5].append(jnp.concatenate([state_conv[l][:, tq:, :], r3(u)], axis=1))
            yc = yc.reshape(-1, D_MODEL)
            o = o.reshape(-1, D_MODEL)
            if l % 2 == 0:
                xn, h = _merge(x, yc, o, ga, attn_w_out_b, w_out_b, gf_all, l)
                x = _ffn(h, xn, ffn_wg, ffn_wu, ffn_wd, l // 2)
            else:
                xn, h, gate = _merge(x, yc, o, ga, attn_w_out_b, w_out_b, gf_all, l, router_p, l // 2)
                x = _moe(h, xn, gate, moe_wg, moe_wu, moe_wd, l // 2)
            new_x.append(x)
        xp, xs = new_x
    fg = final_norm_g.reshape(1, D_MODEL)
    y_prompt = _final_norm(xp, fg).reshape(b, s, D_MODEL)
    y_sample = _final_norm(xs, fg).reshape(db, tq, D_MODEL)
    stacked = [jnp.stack(o) for o in outs]
    return (y_prompt, y_sample, stacked[0], stacked[1], stacked[2], stacked[3], stacked[4], stacked[5])
```

```python
import functools
import math

import numpy as np
import jax
import jax.numpy as jnp
from jax import lax
from jax.experimental import pallas as pl
from jax.experimental.pallas import tpu as pltpu

F32 = jnp.float32
BF16 = jnp.bfloat16

D_MODEL = 1024
N_HEADS = 8
HEAD_DIM = 64
V_DIM = 2 * HEAD_DIM
N_SEG = 7
CONV_WIDTH = 31
HALO = 32
N_BUCKETS = 32
MAX_DISTANCE = 128
PAGE_SIZE = 128
N_EXPERTS = 8
RMS_EPS = 1e-6
SUBLN_EPS = 1e-5
LN_EPS = 1e-5
NEG_INF = -1e30
ATTN_GROUPS = (8, 4, 2, 1)
LOG2E = 1.4426950408889634
LANES = 128
SUBLANES = 8
VMEM_LIMIT = 56 * 1024 * 1024


def _params(*sem):
    return pltpu.CompilerParams(dimension_semantics=sem, vmem_limit_bytes=VMEM_LIMIT)


def _row_tile(m, pref=512):
    return pref if m % pref == 0 else m


def _resident(shape, index_map):
    return pl.BlockSpec(shape, index_map, pipeline_mode=pl.Buffered(1))


def _rms(x, g, eps):
    return x * lax.rsqrt(jnp.mean(x * x, axis=-1, keepdims=True) + eps) * g


def _bias_body(ids_ref, texp_ref, o_ref):
    ids = ids_ref[0]
    acc = jnp.where(ids < 0, NEG_INF, 0.0).astype(F32)
    for n in range(N_BUCKETS):
        acc = jnp.where(ids == n, texp_ref[0, n:n + 1, :], acc)
    o_ref[0, 0] = acc


def _bias_tiles(ids, texp):
    k, r, c = ids.shape
    g = texp.shape[0]
    return pl.pallas_call(
        _bias_body,
        out_shape=jax.ShapeDtypeStruct((g, k, r, c), F32),
        grid=(g, k),
        in_specs=[pl.BlockSpec((1, r, c), lambda a, b: (b, 0, 0)),
                  pl.BlockSpec((1, N_BUCKETS, c), lambda a, b: (a, 0, 0))],
        out_specs=pl.BlockSpec((1, 1, r, c), lambda a, b: (a, b, 0, 0)),
        compiler_params=_params("arbitrary", "arbitrary"),
        name="bias_tiles",
    )(ids, texp)


def _bucket_np(n):
    n = np.maximum(n, 0)
    max_exact = N_BUCKETS // 2
    nf = np.maximum(n, 1).astype(np.float32)
    large = max_exact + (np.log(nf / max_exact) / math.log(MAX_DISTANCE / max_exact)
                         * (N_BUCKETS - max_exact)).astype(np.int32)
    large = np.minimum(large, N_BUCKETS - 1)
    return np.where(n < max_exact, n, large).astype(np.int32)


def _inproj_body(x_ref, g_ref, w_ref, u_ref, q_ref, k_ref, v_ref, kb_ref, vb_ref, gc_ref, ga_ref):
    h = _rms(x_ref[...], g_ref[...], RMS_EPS).astype(BF16)

    def seg(c):
        return jnp.dot(h, w_ref[0, :, c * D_MODEL:(c + 1) * D_MODEL], preferred_element_type=F32)

    u_ref[...] = seg(0) * jax.nn.sigmoid(seg(1))
    q_ref[...] = (seg(2) * HEAD_DIM ** -0.5).astype(BF16)
    k = seg(3)
    k_ref[...] = k
    kb_ref[...] = k.astype(BF16)
    v = seg(4)
    v_ref[...] = v
    vb_ref[...] = v.astype(BF16)
    gc_ref[...] = jax.nn.sigmoid(seg(5))
    ga_ref[...] = jax.nn.sigmoid(seg(6))


def _inproj(x, g, w_all, layer):
    m = x.shape[0]
    tm = _row_tile(m, 256)
    row = lambda i: (i, 0)
    blk = pl.BlockSpec((tm, D_MODEL), row)
    sd = lambda dt: jax.ShapeDtypeStruct((m, D_MODEL), dt)
    return pl.pallas_call(
        _inproj_body,
        out_shape=(sd(F32), sd(BF16), sd(F32), sd(F32), sd(BF16), sd(BF16), sd(F32), sd(F32)),
        grid=(m // tm,),
        in_specs=[blk,
                  _resident((1, D_MODEL), lambda i: (0, 0)),
                  _resident((1, D_MODEL, N_SEG * D_MODEL), lambda i: (layer, 0, 0))],
        out_specs=(blk,) * 8,
        compiler_params=_params("parallel"),
        name="inproj",
    )(x, g, w_all)


def _conv_body(u_ref, halo_ref, hist_ref, dw_ref, db_ref, lg_ref, lb_ref, wpw_ref, gc_ref,
               o_ref, win_ref, cv_ref, *, bb, tm, rc):
    first = pl.program_id(1) == 0
    rnd = lambda a: a.astype(BF16).astype(F32)
    win_ref[:, HALO:HALO + tm, :] = rnd(u_ref[...])
    win_ref[:, HALO + tm:, :] = jnp.zeros((bb, SUBLANES, D_MODEL), F32)

    @pl.when(first)
    def _():
        win_ref[:, 0:HALO, :] = rnd(hist_ref[...])

    @pl.when(jnp.logical_not(first))
    def _():
        win_ref[:, 0:HALO, :] = rnd(halo_ref[...])

    n_chunks = tm // rc
    span = rc + HALO + SUBLANES

    def chunk(idx, carry):
        b = idx // n_chunks
        r0 = pl.multiple_of((idx % n_chunks) * rc, SUBLANES)
        for lc in range(D_MODEL // LANES):
            ls = slice(lc * LANES, (lc + 1) * LANES)
            x = win_ref[b, pl.ds(r0, span), ls]
            acc = jnp.zeros((rc, LANES), F32)
            for s in range(SUBLANES):
                xs = x if s == 0 else pltpu.roll(x, span - s, axis=0)
                for a in range(HALO // SUBLANES + 1):
                    j = SUBLANES * a + s - (HALO - CONV_WIDTH + 1)
                    if 0 <= j < CONV_WIDTH:
                        acc = acc + rnd(dw_ref[0, j:j + 1, ls]) * xs[SUBLANES * a:SUBLANES * a + rc]
            cv_ref[pl.ds(pl.multiple_of(b * tm + r0, SUBLANES), rc), ls] = acc
        return carry

    lax.fori_loop(0, bb * n_chunks, chunk, 0)

    y = cv_ref[...] + db_ref[...]
    mu = jnp.mean(y, axis=-1, keepdims=True)
    yc = y - mu
    var = jnp.mean(yc * yc, axis=-1, keepdims=True)
    y = yc * lax.rsqrt(var + LN_EPS) * lg_ref[...] + lb_ref[...]
    y = (y * jax.nn.sigmoid(y)).astype(BF16)
    z = jnp.dot(y, wpw_ref[0], preferred_element_type=F32)
    o_ref[...] = (gc_ref[...].reshape(bb * tm, D_MODEL) * z).reshape(bb, tm, D_MODEL)


def _conv(u, hist, dw, db, lg, lb, wpw_all, gc, layer, *, bb, tm):
    nb, s, c = u.shape
    rc = min(tm, 64)
    n_t = s // tm
    if n_t > 1:
        halo_src = u
        per = tm // HALO
        halo_map = lambda b, i: (b, jnp.maximum(i * per - 1, 0), 0)
    else:
        halo_src = hist
        halo_map = lambda b, i: (b, 0, 0)
    vec = lambda: _resident((1, 1, c), lambda b, i: (layer, 0, 0))
    return pl.pallas_call(
        functools.partial(_conv_body, bb=bb, tm=tm, rc=rc),
        out_shape=jax.ShapeDtypeStruct((nb, s, c), F32),
        grid=(nb // bb, n_t),
        in_specs=[pl.BlockSpec((bb, tm, c), lambda b, i: (b, i, 0)),
                  pl.BlockSpec((bb, HALO, c), halo_map),
                  pl.BlockSpec((bb, HALO, c), lambda b, i: (b, 0, 0)),
                  _resident((1, CONV_WIDTH, c), lambda b, i: (layer, 0, 0)),
                  vec(), vec(), vec(),
                  _resident((1, c, D_MODEL), lambda b, i: (layer, 0, 0)),
                  pl.BlockSpec((bb, tm, c), lambda b, i: (b, i, 0))],
        out_specs=pl.BlockSpec((bb, tm, c), lambda b, i: (b, i, 0)),
        scratch_shapes=[pltpu.VMEM((bb, tm + HALO + SUBLANES, c), F32),
                        pltpu.VMEM((bb * tm, c), F32)],
        compiler_params=_params("parallel", "arbitrary"),
        name="conv",
    )(u, halo_src, hist, dw, db, lg, lb, wpw_all, gc)


def _lambda(lam_ref, lam_init):
    t = lam_ref[0]
    a = jnp.sum(t[0:1] * t[1:2], axis=-1, keepdims=True)
    b = jnp.sum(t[2:3] * t[3:4], axis=-1, keepdims=True)
    return jnp.exp(a) - jnp.exp(b) + lam_init


def _subln(o, sg, lam_init):
    return _rms(o, sg, SUBLN_EPS) * (1.0 - lam_init)


def _attn_prompt_body(tbl_ref, lam_ref, sg_ref, q_ref, k_ref, v_ref, bias_ref, o_ref,
                      s_ref, red_ref, m_ref, off_ref, c_ref, acc_ref, *, t, lam_init):
    h = pl.program_id(1)
    i = pl.program_id(2)
    q = q_ref[0]
    lane = lax.broadcasted_iota(jnp.int32, (t, V_DIM), 1)
    zero = jnp.zeros((t, V_DIM), BF16)
    qs = jnp.concatenate([jnp.where(lane < HEAD_DIM, q, zero), jnp.where(lane >= HEAD_DIM, q, zero)], axis=0)
    far = tbl_ref[N_BUCKETS - 1, h]
    nl = t // LANES
    wide = (2 * t, LANES)

    def scores(j, n=1):
        k = k_ref[0, pl.ds(pl.multiple_of(j * t, t), n * t), :]
        return lax.dot_general(qs, k, (((1,), (1,)), ((), ())), preferred_element_type=F32)

    def fold(x, op):
        r = x[:, 0:LANES]
        for c in range(1, x.shape[1] // LANES):
            r = op(r, x[:, c * LANES:(c + 1) * LANES])
        return r

    def grouped(n_tiles, body):
        big = ATTN_GROUPS[0]
        n_big = n_tiles // big

        def group_step(g, carry):
            body(g * big, big)
            return carry

        lax.fori_loop(0, n_big, group_step, 0)
        start = n_big * big
        rem = n_tiles - start
        for size in ATTN_GROUPS[1:]:
            take = (rem & size) != 0

            @pl.when(take)
            def _(start=start, size=size):
                body(start, size)

            start = start + jnp.where(take, size, 0)

    def rep(x):
        return jnp.concatenate([x] * nl, axis=1)

    red_ref[...] = jnp.full(wide, NEG_INF, F32)

    n_far = jnp.maximum(i - 1, 0)

    def far_a(j, n):
        s = scores(j, n) * LOG2E
        for c in range(n):
            s_ref[j + c] = s[:, c * t:(c + 1) * t]
        red_ref[...] = jnp.maximum(red_ref[...], fold(s, jnp.maximum))

    grouped(n_far, far_a)
    far2 = far * LOG2E
    red_ref[...] = red_ref[...] + far2

    def near_a(j, n):
        bias = bias_ref[0, 0] if n == 1 else jnp.concatenate([bias_ref[0, 1], bias_ref[0, 0]], axis=1)
        s = (scores(j, n) + jnp.concatenate([bias, bias], axis=0)) * LOG2E
        for c in range(n):
            s_ref[j + c] = s[:, c * t:(c + 1) * t]
        red_ref[...] = jnp.maximum(red_ref[...], fold(s, jnp.maximum))

    def on_last_tiles(body):
        @pl.when(i >= 1)
        def _():
            body(i - 1, 2)

        @pl.when(i == 0)
        def _():
            body(0, 1)

    on_last_tiles(near_a)
    m_ref[...] = jnp.broadcast_to(jnp.max(red_ref[...], axis=-1, keepdims=True), wide)

    red_ref[...] = jnp.zeros(wide, F32)
    off_ref[...] = m_ref[...] - far2

    def expo(j, off):
        e = jnp.exp2(s_ref[j] - rep(off))
        s_ref[j] = e
        red_ref[...] = red_ref[...] + fold(e, jnp.add)

    def far_b(j, n):
        for c in range(n):
            expo(j + c, off_ref[...])

    grouped(n_far, far_b)

    def near_b(j, n):
        for c in range(n):
            expo(j + c, m_ref[...])

    on_last_tiles(near_b)

    inv = 1.0 / jnp.sum(red_ref[...], axis=-1, keepdims=True)
    lam = _lambda(lam_ref, lam_init)
    c_ref[0:t, :] = jnp.broadcast_to(inv[0:t], (t, LANES))
    c_ref[t:2 * t, :] = jnp.broadcast_to(lam * inv[t:2 * t], (t, LANES))

    acc_ref[...] = jnp.zeros((t, V_DIM), F32)

    def pv(j, n):
        c1, c2 = rep(c_ref[0:t, :]), rep(c_ref[t:2 * t, :])
        a = [(s_ref[j + c, 0:t, :] * c1 - s_ref[j + c, t:2 * t, :] * c2).astype(BF16) for c in range(n)]
        v = v_ref[0, pl.ds(pl.multiple_of(j * t, t), n * t), :]
        acc_ref[...] += jnp.dot(jnp.concatenate(a, axis=1), v, preferred_element_type=F32)

    grouped(i + 1, pv)
    o_ref[0] = _subln(acc_ref[...], sg_ref[0], lam_init).astype(BF16)


def _attn_prompt(q, k, v, bias, table, lam_all, sg_all, layer, lam_init, t):
    b, s, _ = q.shape
    smem = pl.BlockSpec(memory_space=pltpu.SMEM)
    wide = (2 * t, LANES)
    return pl.pallas_call(
        functools.partial(_attn_prompt_body, t=t, lam_init=lam_init),
        out_shape=jax.ShapeDtypeStruct((b, s, N_HEADS * V_DIM), BF16),
        grid=(b, N_HEADS, s // t),
        in_specs=[smem,
                  pl.BlockSpec((1, 4, HEAD_DIM), lambda bi, h, i: (layer, 0, 0)),
                  pl.BlockSpec((1, 1, V_DIM), lambda bi, h, i: (layer, 0, 0)),
                  pl.BlockSpec((1, t, V_DIM), lambda bi, h, i: (bi, i, h)),
                  pl.BlockSpec((1, s, V_DIM), lambda bi, h, i: (bi, 0, h)),
                  pl.BlockSpec((1, s, V_DIM), lambda bi, h, i: (bi, 0, h)),
                  pl.BlockSpec((1, 2, t, t), lambda bi, h, i: (h, 0, 0, 0))],
        out_specs=pl.BlockSpec((1, t, V_DIM), lambda bi, h, i: (bi, i, h)),
        scratch_shapes=[pltpu.VMEM((s // t, 2 * t, t), F32),
                        pltpu.VMEM(wide, F32), pltpu.VMEM(wide, F32), pltpu.VMEM(wide, F32), pltpu.VMEM(wide, F32),
                        pltpu.VMEM((t, V_DIM), F32)],
        compiler_params=_params("parallel", "parallel", "arbitrary"),
        name="attn_prompt",
    )(table, lam_all, sg_all, q, k, v, bias)


N_COLS = N_HEADS * 2 * 8


def _attn_sample_body(pt_ref, lam_ref, sg_ref, qbd_ref, bias_ref, kn_ref, vn_ref, *rest,
                      g, n_steps, lam_init):
    k_refs = rest[:g]
    v_refs = rest[g:2 * g]
    o_ref, m_ref, l_ref, a_ref = rest[2 * g:]
    step = pl.program_id(1)

    @pl.when(step == 0)
    def _():
        m_ref[...] = jnp.full((1, N_COLS), NEG_INF, F32)
        l_ref[...] = jnp.zeros((1, N_COLS), F32)
        a_ref[...] = jnp.zeros((N_COLS, V_DIM), F32)

    qbd = qbd_ref[0]

    def wide(ref):
        return jnp.concatenate(
            [ref[0, 0, pl.ds(hh, PAGE_SIZE, stride=N_HEADS), :] for hh in range(N_HEADS)], axis=1)

    def scores(kw, bias):
        return jnp.dot(kw, qbd, preferred_element_type=F32) + bias

    def accumulate(s_list, v_list):
        m_prev = m_ref[...]
        m_new = m_prev
        for s in s_list:
            m_new = jnp.maximum(m_new, jnp.max(s, axis=0, keepdims=True))
        alpha = jnp.exp(m_prev - m_new)
        l_new = alpha * l_ref[...]
        upd = jnp.zeros((N_COLS, V_DIM), F32)
        for s, vw in zip(s_list, v_list):
            p = jnp.exp(s - m_new)
            l_new = l_new + jnp.sum(p, axis=0, keepdims=True)
            full = jnp.dot(p.T, vw, preferred_element_type=F32)
            upd = upd + jnp.concatenate(
                [full[hh * 16:(hh + 1) * 16, hh * V_DIM:(hh + 1) * V_DIM] for hh in range(N_HEADS)], axis=0)
        alpha_col = jnp.broadcast_to(alpha, (N_COLS, N_COLS)).T
        a_ref[...] = alpha_col * a_ref[...] + upd
        l_ref[...] = l_new
        m_ref[...] = m_new

    far = bias_ref[0, 0, 0:1, :]
    near = bias_ref[0, 0, SUBLANES:SUBLANES + PAGE_SIZE, :]
    self_bias = bias_ref[0, 0, SUBLANES + PAGE_SIZE:, :]

    @pl.when(step < n_steps - 1)
    def _():
        accumulate([scores(wide(r), far) for r in k_refs], [wide(r) for r in v_refs])

    @pl.when(step == n_steps - 1)
    def _():
        s_list = [scores(wide(r), far) for r in k_refs[:-1]]
        s_list.append(scores(wide(k_refs[-1]), near))
        accumulate(s_list, [wide(r) for r in v_refs])
        pad = jnp.zeros((PAGE_SIZE - 8, N_HEADS * V_DIM), F32)
        kn = jnp.concatenate([kn_ref[0], pad], axis=0)
        vn = jnp.concatenate([vn_ref[0], pad], axis=0)
        accumulate([scores(kn, self_bias)], [vn])
        lam = _lambda(lam_ref, lam_init)
        inv_l = jnp.broadcast_to(1.0 / l_ref[...], (N_COLS, N_COLS)).T
        acc = a_ref[...] * inv_l
        for hh in range(N_HEADS):
            o = acc[hh * 16:hh * 16 + 8] - lam * acc[hh * 16 + 8:hh * 16 + 16]
            o_ref[0, :, hh * V_DIM:(hh + 1) * V_DIM] = _subln(o, sg_ref[0], lam_init).astype(BF16)


def _attn_sample(q, k_new, v_new, cache_k, cache_v, page_table, bias, lam_all, sg_all, layer, lam_init):
    db, tq, _ = q.shape
    n_pages = page_table.shape[1]
    g = 8 if n_pages % 8 == 0 else n_pages
    n_steps = n_pages // g
    q5 = q.astype(F32).reshape(db, tq, N_HEADS, V_DIM).transpose(0, 2, 3, 1)
    half = (jnp.arange(V_DIM)[:, None] >= HEAD_DIM) == (jnp.arange(2)[None, :] == 1)
    qm = q5[:, :, :, None, :] * half.astype(F32)[None, None, :, :, None]
    eye = jnp.eye(N_HEADS, dtype=F32)
    qbd = (qm[:, :, :, None, :, :] * eye[None, :, None, :, None, None]).reshape(db, N_HEADS * V_DIM, N_COLS)

    def page_map(j):
        return lambda b, s, pt: (layer, pt[b * n_pages + s * g + j], 0, 0)

    page_blk = (1, 1, PAGE_SIZE * N_HEADS, V_DIM)
    in_specs = [pl.BlockSpec((1, 4, HEAD_DIM), lambda b, s, pt: (layer, 0, 0)),
                pl.BlockSpec((1, 1, V_DIM), lambda b, s, pt: (layer, 0, 0)),
                pl.BlockSpec((1, N_HEADS * V_DIM, N_COLS), lambda b, s, pt: (b, 0, 0)),
                pl.BlockSpec((1, 1) + bias.shape[2:], lambda b, s, pt: (0, 0, 0, 0)),
                pl.BlockSpec((1, tq, N_HEADS * V_DIM), lambda b, s, pt: (b, 0, 0)),
                pl.BlockSpec((1, tq, N_HEADS * V_DIM), lambda b, s, pt: (b, 0, 0))]
    in_specs += [pl.BlockSpec(page_blk, page_map(j)) for j in range(g)]
    in_specs += [pl.BlockSpec(page_blk, page_map(j)) for j in range(g)]
    return pl.pallas_call(
        functools.partial(_attn_sample_body, g=g, n_steps=n_steps, lam_init=lam_init),
        out_shape=jax.ShapeDtypeStruct((db, tq, N_HEADS * V_DIM), BF16),
        grid_spec=pltpu.PrefetchScalarGridSpec(
            num_scalar_prefetch=1,
            grid=(db, n_steps),
            in_specs=in_specs,
            out_specs=pl.BlockSpec((1, tq, N_HEADS * V_DIM), lambda b, s, pt: (b, 0, 0)),
            scratch_shapes=[pltpu.VMEM((1, N_COLS), F32), pltpu.VMEM((1, N_COLS), F32),
                            pltpu.VMEM((N_COLS, V_DIM), F32)]),
        compiler_params=_params("parallel", "arbitrary"),
        name="attn_sample",
    )(page_table.reshape(-1), lam_all, sg_all, qbd, bias, k_new, v_new,
      *([cache_k] * g), *([cache_v] * g))


def _top2_gate(logits):
    lane = lax.broadcasted_iota(jnp.int32, logits.shape, 1)
    lg = jnp.where(lane < N_EXPERTS, logits, -jnp.inf)
    m1 = jnp.max(lg, axis=-1, keepdims=True)
    i1 = jnp.min(jnp.where(lg == m1, lane, LANES), axis=-1, keepdims=True)
    lg2 = jnp.where(lane == i1, -jnp.inf, lg)
    m2 = jnp.max(lg2, axis=-1, keepdims=True)
    i2 = jnp.min(jnp.where(lg2 == m2, lane, LANES), axis=-1, keepdims=True)
    e2 = jnp.exp(m2 - m1)
    w1 = 1.0 / (1.0 + e2)
    w2 = e2 / (1.0 + e2)
    return jnp.where(lane == i1, w1, 0.0) + jnp.where(lane == i2, w2, 0.0)


def _merge_body(x_ref, yc_ref, o_ref, ga_ref, wao_ref, wout_ref, gf_ref, *rest, routed):
    if routed:
        wr_ref, xn_ref, h_ref, gate_ref = rest
    else:
        xn_ref, h_ref = rest
    ya = jnp.dot(o_ref[...], wao_ref[0], preferred_element_type=F32)
    mixed = yc_ref[...] + ga_ref[...] * ya
    xn = x_ref[...] + jnp.dot(mixed.astype(BF16), wout_ref[0], preferred_element_type=F32)
    xn_ref[...] = xn
    h = _rms(xn, gf_ref[0], RMS_EPS)
    hb = h.astype(BF16)
    h_ref[...] = hb
    if routed:
        gate_ref[...] = _top2_gate(jnp.dot(hb, wr_ref[0], preferred_element_type=F32))


def _merge(x, yc, o, ga, wao_all, wout_all, gf_all, layer, router_all=None, ridx=0):
    m = x.shape[0]
    tm = _row_tile(m)
    routed = router_all is not None
    blk = pl.BlockSpec((tm, D_MODEL), lambda i: (i, 0))
    wspec = lambda: _resident((1, D_MODEL, D_MODEL), lambda i: (layer, 0, 0))
    in_specs = [blk, blk, blk, blk, wspec(), wspec(), _resident((1, 1, D_MODEL), lambda i: (layer, 0, 0))]
    args = [x, yc, o, ga, wao_all, wout_all, gf_all]
    out_shape = [jax.ShapeDtypeStruct((m, D_MODEL), F32), jax.ShapeDtypeStruct((m, D_MODEL), BF16)]
    out_specs = [blk, blk]
    if routed:
        in_specs.append(_resident((1, D_MODEL, LANES), lambda i: (ridx, 0, 0)))
        args.append(router_all)
        out_shape.append(jax.ShapeDtypeStruct((m, LANES), F32))
        out_specs.append(pl.BlockSpec((tm, LANES), lambda i: (i, 0)))
    return pl.pallas_call(
        functools.partial(_merge_body, routed=routed),
        out_shape=tuple(out_shape),
        grid=(m // tm,),
        in_specs=in_specs,
        out_specs=tuple(out_specs),
        compiler_params=_params("parallel"),
        name="merge",
    )(*args)


def _ffn_body(h_ref, x_ref, wg_ref, wu_ref, wd_ref, o_ref):
    h = h_ref[...]
    a = jnp.dot(h, wg_ref[0], preferred_element_type=F32)
    b = jnp.dot(h, wu_ref[0], preferred_element_type=F32)
    hid = (a * jax.nn.sigmoid(a) * b).astype(BF16)
    o_ref[...] = x_ref[...] + jnp.dot(hid, wd_ref[0], preferred_element_type=F32)


def _ffn(h, x, wg_all, wu_all, wd_all, idx):
    m = x.shape[0]
    tm = _row_tile(m)
    f = wg_all.shape[2]
    blk = pl.BlockSpec((tm, D_MODEL), lambda i: (i, 0))
    return pl.pallas_call(
        _ffn_body,
        out_shape=jax.ShapeDtypeStruct((m, D_MODEL), F32),
        grid=(m // tm,),
        in_specs=[blk, blk,
                  _resident((1, D_MODEL, f), lambda i: (idx, 0, 0)),
                  _resident((1, D_MODEL, f), lambda i: (idx, 0, 0)),
                  _resident((1, f, D_MODEL), lambda i: (idx, 0, 0))],
        out_specs=blk,
        compiler_params=_params("parallel"),
        name="ffn",
    )(h, x, wg_all, wu_all, wd_all)


MOE_ROWS = 128


def _moe_body(h_ref, x_ref, gate_ref, wg_ref, wu_ref, wd_ref, o_ref,
              rank_ref, rank_t_ref, xc_ref, yc_ref, acc_ref, *, n_f, tm):
    e = pl.program_id(1)
    f = pl.program_id(2)

    @pl.when(jnp.logical_and(e == 0, f == 0))
    def _():
        acc_ref[...] = jnp.zeros(acc_ref.shape, F32)
        routed = gate_ref[...] > 0.0
        before = lax.broadcasted_iota(jnp.int32, (1, tm), 1) < lax.broadcasted_iota(jnp.int32, (tm, 1), 0)
        tri = jnp.where(before, 1.0, 0.0).astype(BF16)
        sel = jnp.where(routed, 1.0, 0.0).astype(BF16)
        rank = jnp.where(routed, jnp.dot(tri, sel, preferred_element_type=F32), -1.0)
        rank_ref[...] = rank
        rank_t_ref[...] = rank.T

    rank_row = rank_t_ref[pl.ds(e, 1), :]
    n_blocks = (jnp.max(rank_row).astype(jnp.int32) + MOE_ROWS) // MOE_ROWS

    def rows(rb):
        return pl.ds(pl.multiple_of(rb * MOE_ROWS, MOE_ROWS), MOE_ROWS)

    @pl.when(f == 0)
    def _():
        def compact(rb, carry):
            want = (rb * MOE_ROWS + lax.broadcasted_iota(jnp.int32, (MOE_ROWS, 1), 0)).astype(F32)
            pick = jnp.where(rank_row == want, 1.0, 0.0).astype(BF16)
            xc_ref[rows(rb), :] = jnp.dot(pick, h_ref[...], preferred_element_type=F32).astype(BF16)
            yc_ref[rows(rb), :] = jnp.zeros((MOE_ROWS, D_MODEL), F32)
            return carry

        lax.fori_loop(0, n_blocks, compact, 0)

    def expert(rb, carry):
        xb = xc_ref[rows(rb), :]
        a = jnp.dot(xb, wg_ref[0, 0], preferred_element_type=F32)
        b = jnp.dot(xb, wu_ref[0, 0], preferred_element_type=F32)
        hid = (a * jax.nn.sigmoid(a) * b).astype(BF16)
        yc_ref[rows(rb), :] += jnp.dot(hid, wd_ref[0, 0], preferred_element_type=F32)
        return carry

    lax.fori_loop(0, n_blocks, expert, 0)

    @pl.when(f == n_f - 1)
    def _():
        lane = lax.broadcasted_iota(jnp.int32, (tm, LANES), 1)
        mine = lane == e
        rank_col = jnp.sum(jnp.where(mine, rank_ref[...], 0.0), axis=-1, keepdims=True)
        ge = jnp.sum(jnp.where(mine, gate_ref[...], 0.0), axis=-1, keepdims=True)

        def scatter(rb, carry):
            want = (rb * MOE_ROWS + lax.broadcasted_iota(jnp.int32, (1, MOE_ROWS), 1)).astype(F32)
            pick = jnp.where(rank_col == want, 1.0, 0.0).astype(BF16)
            y = jnp.dot(pick, yc_ref[rows(rb), :].astype(BF16), preferred_element_type=F32)
            acc_ref[...] += ge * y
            return carry

        lax.fori_loop(0, n_blocks, scatter, 0)

    @pl.when(jnp.logical_and(e == N_EXPERTS - 1, f == n_f - 1))
    def _():
        o_ref[...] = x_ref[...] + acc_ref[...]


def _moe(h, x, gate, wg_all, wu_all, wd_all, idx):
    m = x.shape[0]
    tm = _row_tile(m, 1024)
    fe = wg_all.shape[3]
    tf = fe // 2
    n_f = fe // tf
    assert tm % MOE_ROWS == 0
    blk = lambda: _resident((tm, D_MODEL), lambda i, e, f: (i, 0))
    return pl.pallas_call(
        functools.partial(_moe_body, n_f=n_f, tm=tm),
        out_shape=jax.ShapeDtypeStruct((m, D_MODEL), F32),
        grid=(m // tm, N_EXPERTS, n_f),
        in_specs=[blk(), blk(),
                  pl.BlockSpec((tm, LANES), lambda i, e, f: (i, 0)),
                  pl.BlockSpec((1, 1, D_MODEL, tf), lambda i, e, f: (idx, e, 0, f)),
                  pl.BlockSpec((1, 1, D_MODEL, tf), lambda i, e, f: (idx, e, 0, f)),
                  pl.BlockSpec((1, 1, tf, D_MODEL), lambda i, e, f: (idx, e, f, 0))],
        out_specs=pl.BlockSpec((tm, D_MODEL), lambda i, e, f: (i, 0)),
        scratch_shapes=[pltpu.VMEM((tm, LANES), F32), pltpu.VMEM((LANES, tm), F32),
                        pltpu.VMEM((tm, D_MODEL), BF16), pltpu.VMEM((tm, D_MODEL), F32),
                        pltpu.VMEM((tm, D_MODEL), F32)],
        compiler_params=_params("parallel", "arbitrary", "arbitrary"),
        name="moe",
    )(h, x, gate, wg_all, wu_all, wd_all)


def _final_body(x_ref, g_ref, o_ref):
    o_ref[...] = _rms(x_ref[...], g_ref[...], RMS_EPS)


def _final_norm(x, g):
    m = x.shape[0]
    tm = _row_tile(m)
    blk = pl.BlockSpec((tm, D_MODEL), lambda i: (i, 0))
    return pl.pallas_call(
        _final_body,
        out_shape=jax.ShapeDtypeStruct((m, D_MODEL), F32),
        grid=(m // tm,),
        in_specs=[blk, _resident((1, D_MODEL), lambda i: (0, 0))],
        out_specs=blk,
        compiler_params=_params("parallel"),
        name="final_norm",
    )(x, g)


def _prompt_bias_ids(t):
    qq = np.arange(t)[:, None]
    kk = np.arange(t)[None, :]
    diag = np.where(kk <= qq, _bucket_np(qq - kk), -1)
    below = _bucket_np(t + qq - kk)
    return np.stack([diag, below]).astype(np.int32)


def _sample_bias_ids(past_len, tq):
    tcol = np.tile(np.arange(tq), N_HEADS * 2)[None, :]
    far = np.full((SUBLANES, N_COLS), N_BUCKETS - 1)
    kpos = past_len - PAGE_SIZE + np.arange(PAGE_SIZE)[:, None]
    near = _bucket_np(past_len + tcol - kpos)
    tk = np.arange(PAGE_SIZE)[:, None]
    self_ids = np.where((tk < tq) & (tk <= tcol), _bucket_np(tcol - tk), -1)
    return np.concatenate([far, near, self_ids])[None].astype(np.int32)


def kernel(x_prompt, x_sample, cache_k, cache_v, state_conv, page_table, w_in, mix_norm_g, conv_dw_w, conv_dw_b, conv_ln_g, conv_ln_b, conv_w_out, lambda_q1, lambda_k1, lambda_q2, lambda_k2, subln_g, attn_w_out, rel_bias_table, w_out, ffn_norm_g, ffn_w_gate, ffn_w_up, ffn_w_down, moe_router, moe_w_gate, moe_w_up, moe_w_down, final_norm_g):
    b, s, _ = x_prompt.shape
    db, tq, _ = x_sample.shape
    depth = w_in.shape[0]
    n_phys = cache_k.shape[1]
    n_pages = page_table.shape[1]
    past_len = n_pages * PAGE_SIZE
    t_attn = min(256, s)
    assert s % t_attn == 0 and t_attn >= PAGE_SIZE and tq == SUBLANES
    assert (_bucket_np(np.arange(min(t_attn, PAGE_SIZE) + 1, max(s, past_len) + tq + 1)) == N_BUCKETS - 1).all()

    w_in_b = w_in.astype(BF16)
    conv_w_out_b = conv_w_out.astype(BF16)
    attn_w_out_b = attn_w_out.astype(BF16)
    w_out_b = w_out.astype(BF16)
    ffn_wg, ffn_wu, ffn_wd = (w.astype(BF16) for w in (ffn_w_gate, ffn_w_up, ffn_w_down))
    moe_wg, moe_wu, moe_wd = (w.astype(BF16) for w in (moe_w_gate, moe_w_up, moe_w_down))
    router_p = jnp.pad(moe_router, ((0, 0), (0, 0), (0, LANES - N_EXPERTS))).astype(BF16)
    lam_all = jnp.stack([lambda_q1, lambda_k1, lambda_q2, lambda_k2], axis=1)
    vec3 = lambda a: a.reshape(depth, 1, -1)
    sg_all, gf_all = vec3(subln_g), vec3(ffn_norm_g)
    db_all, lg_all, lb_all = vec3(conv_dw_b), vec3(conv_ln_g), vec3(conv_ln_b)
    cache_k4 = cache_k.reshape(depth, n_phys, PAGE_SIZE * N_HEADS, V_DIM)
    cache_v4 = cache_v.reshape(depth, n_phys, PAGE_SIZE * N_HEADS, V_DIM)

    texp_p = jnp.broadcast_to(rel_bias_table.T[:, :, None], (N_HEADS, N_BUCKETS, t_attn))
    bias_p = _bias_tiles(jnp.asarray(_prompt_bias_ids(t_attn)), texp_p)
    texp_s = jnp.repeat(rel_bias_table, N_COLS // N_HEADS, axis=1)[None]
    bias_s = _bias_tiles(jnp.asarray(_sample_bias_ids(past_len, tq)), texp_s)

    hist_p = jnp.zeros((b, HALO, D_MODEL), F32)
    xp = x_prompt.reshape(b * s, D_MODEL)
    xs = x_sample.reshape(db * tq, D_MODEL)
    outs = [[] for _ in range(6)]
    for l in range(depth):
        lam_init = 0.8 - 0.6 * math.exp(-0.3 * l)
        g_mix = mix_norm_g[l].reshape(1, D_MODEL)
        new_x = []
        for grp, x in enumerate((xp, xs)):
            u, q, k, v, kb, vb, gc, ga = _inproj(x, g_mix, w_in_b, l)
            if grp == 0:
                r3 = lambda a: a.reshape(b, s, D_MODEL)
                yc = _conv(r3(u), hist_p, conv_dw_w, db_all, lg_all, lb_all, conv_w_out_b, r3(gc), l,
                           bb=1, tm=min(512, s))
                o = _attn_prompt(r3(q), r3(kb), r3(vb), bias_p, rel_bias_table, lam_all, sg_all, l, lam_init, t_attn)
                n_pg = b * s // PAGE_SIZE
                outs[0].append(k.reshape(n_pg, PAGE_SIZE, N_HEADS, V_DIM))
                outs[1].append(v.reshape(n_pg, PAGE_SIZE, N_HEADS, V_DIM))
                outs[2].append(r3(u)[:, s - (CONV_WIDTH - 1):, :])
            else:
                r3 = lambda a: a.reshape(db, tq, D_MODEL)
                hist_s = jnp.concatenate(
                    [jnp.zeros((db, HALO - CONV_WIDTH + 1, D_MODEL), F32), state_conv[l]], axis=1)
                yc = _conv(r3(u), hist_s, conv_dw_w, db_all, lg_all, lb_all, conv_w_out_b, r3(gc), l,
                           bb=min(8, db), tm=tq)
                o = _attn_sample(r3(q), r3(k), r3(v), cache_k4, cache_v4, page_table, bias_s,
                                 lam_all, sg_all, l, lam_init)
                outs[3].append(k.reshape(db, tq, N_HEADS, V_DIM))
                outs[4].append(v.reshape(db, tq, N_HEADS, V_DIM))
                outs[5].append(jnp.concatenate([state_conv[l][:, tq:, :], r3(u)], axis=1))
            yc = yc.reshape(-1, D_MODEL)
            o = o.reshape(-1, D_MODEL)
            if l % 2 == 0:
                xn, h = _merge(x, yc, o, ga, attn_w_out_b, w_out_b, gf_all, l)
                x = _ffn(h, xn, ffn_wg, ffn_wu, ffn_wd, l // 2)
            else:
                xn, h, gate = _merge(x, yc, o, ga, attn_w_out_b, w_out_b, gf_all, l, router_p, l // 2)
                x = _moe(h, xn, gate, moe_wg, moe_wu, moe_wd, l // 2)
            new_x.append(x)
        xp, xs = new_x
    fg = final_norm_g.reshape(1, D_MODEL)
    y_prompt = _final_norm(xp, fg).reshape(b, s, D_MODEL)
    y_sample = _final_norm(xs, fg).reshape(db, tq, D_MODEL)
    stacked = [jnp.stack(o) for o in outs]
    return (y_prompt, y_sample, stacked[0], stacked[1], stacked[2], stacked[3], stacked[4], stacked[5])
```
